```python
import math
import jax, jax.numpy as jnp
from jax import lax
import numpy as np

D_MODEL = 4096
BATCH = 16
SEQ = 256
DEPTH = 2
DEC_BATCH = 8
DEC_SEQ = 4096
PAST_LEN = 256

GRID_W = 64
N_BRANCH = 4
MIX_W = D_MODEL // 4
HEAD_DIM = 64
S5_CH = 16
S5_GROUPS = MIX_W // S5_CH
S5_P = 64
RW_HEADS = MIX_W // HEAD_DIM
RW_N = HEAD_DIM
RW_LORA_W = 64
RW_LORA_A = 64
RW_LORA_G = 128
GN_EPS = 64e-5
NA_HEADS = MIX_W // HEAD_DIM
NA_KH = 8
NA_KW = 16
SWA_HEADS = MIX_W // HEAD_DIM
SWA_KV_HEADS = 4
SWA_GROUP = SWA_HEADS // SWA_KV_HEADS
SWA_WINDOW = 128
ATT_BLOCK = 128
ROPE_BASE = 10000.0
NEG_INF = -1e30
N_EXPERTS = 32
TOP_K = 4
EXPERT_FF = D_MODEL // 2
SWIGLU_LIMIT = 7.0
SWIGLU_ALPHA = 1.702
MOE_BLOCK = 128
RMS_EPS = 1e-6
RW_IN = 3 * MIX_W + 2 * RW_LORA_W + 2 * RW_LORA_A + RW_LORA_G
NA_IN = 3 * MIX_W
SWA_IN = SWA_HEADS * HEAD_DIM + 2 * SWA_KV_HEADS * HEAD_DIM
GATE_IN = N_BRANCH * D_MODEL
N_IN = MIX_W + RW_IN + NA_IN + SWA_IN + GATE_IN

kernel_name = "hybrid_diffusion_prefix_ctx_step"

F32 = jnp.float32


def rms_norm(x, g):
    x32 = x.astype(F32)
    y = x32 * lax.rsqrt(jnp.mean(x32 * x32, axis=-1, keepdims=True) + RMS_EPS)
    return (y * g.astype(F32)).astype(x.dtype)


def _softmax_sink(s, sink):
    sk = jnp.broadcast_to(sink.astype(F32).reshape(SWA_KV_HEADS, SWA_GROUP)[None, :, :, None, None], s.shape[:-1] + (1,))
    p = jax.nn.softmax(jnp.concatenate([s, sk], axis=-1), axis=-1)
    return p[..., :-1]


def _rotate_half(x, ang):
    q = x.shape[-1] // 2
    x1, x2 = x[..., :q], x[..., q:]
    cos, sin = jnp.cos(ang), jnp.sin(ang)
    return jnp.concatenate([x1 * cos - x2 * sin, x2 * cos + x1 * sin], axis=-1)


def rope_2d(x):
    L = x.shape[1]
    quarter = HEAD_DIM // 4
    t = jnp.arange(L)
    inv = ROPE_BASE ** (-jnp.arange(quarter, dtype=F32) / quarter)
    shape = (1, L) + (1,) * (x.ndim - 3) + (quarter,)
    ang_r = ((t // GRID_W).astype(F32)[:, None] * inv).reshape(shape)
    ang_c = ((t % GRID_W).astype(F32)[:, None] * inv).reshape(shape)
    x32 = x.astype(F32)
    half = HEAD_DIM // 2
    out = jnp.concatenate([_rotate_half(x32[..., :half], ang_r), _rotate_half(x32[..., half:], ang_c)], axis=-1)
    return out.astype(x.dtype)


def _cmul(ar, ai, br, bi):
    return ar * br - ai * bi, ar * bi + ai * br


def _lin_combine(e1, e2):
    a1r, a1i, b1r, b1i = e1
    a2r, a2i, b2r, b2i = e2
    ar, ai = _cmul(a2r, a2i, a1r, a1i)
    br, bi = _cmul(a2r, a2i, b1r, b1i)
    return ar, ai, br + b2r, bi + b2i


def s5_mixer(u, h0, lam_re, lam_im, log_dt, b_re, b_im, c_re, c_im, d_skip, glu_w, glu_b):
    bsz, L, _ = u.shape
    u32 = u.astype(F32)
    ug = u32.reshape(bsz, L, S5_GROUPS, S5_CH)
    y = d_skip.astype(F32) * u32
    finals = []
    for dr in range(2):
        lr, li = lam_re[dr].astype(F32), lam_im[dr].astype(F32)
        dt = jnp.exp(log_dt[dr].astype(F32))[:, None]
        mag = jnp.exp(lr * dt)
        abr, abi = mag * jnp.cos(li * dt), mag * jnp.sin(li * dt)
        den = lr * lr + li * li
        cr, ci = _cmul(abr - 1.0, abi, lr / den, -li / den)
        bbr, bbi = _cmul(cr[..., None], ci[..., None], b_re[dr].astype(F32), b_im[dr].astype(F32))
        xs = ug if dr == 0 else jnp.flip(ug, axis=1)
        bur = jnp.einsum("gpc,blgc->blgp", bbr, xs)
        bui = jnp.einsum("gpc,blgc->blgp", bbi, xs)
        if h0 is not None:
            pr, pi = _cmul(abr, abi, h0[:, dr, :, :, 0].astype(F32), h0[:, dr, :, :, 1].astype(F32))
            bur = bur.at[:, 0].add(pr)
            bui = bui.at[:, 0].add(pi)
        ar = jnp.broadcast_to(abr, (1, L) + abr.shape)
        ai = jnp.broadcast_to(abi, (1, L) + abi.shape)
        _, _, hr, hi = lax.associative_scan(_lin_combine, (ar, ai, bur, bui), axis=1)
        if h0 is None:
            finals.append(jnp.stack([hr[:, -1], hi[:, -1]], axis=-1))
        yd = (jnp.einsum("gcp,blgp->blgc", c_re[dr].astype(F32), hr)
              - jnp.einsum("gcp,blgp->blgc", c_im[dr].astype(F32), hi)).reshape(bsz, L, MIX_W)
        y = y + (yd if dr == 0 else jnp.flip(yd, axis=1))
    g = jax.nn.gelu(y)
    out = g * jax.nn.sigmoid(g @ glu_w.astype(F32) + glu_b.astype(F32))
    state = jnp.stack(finals, axis=1) if h0 is None else None
    return out.astype(u.dtype), state


def _shift_centred(x):
    zero = jnp.zeros_like(x[:, :1])
    prev = jnp.concatenate([zero, x[:, :-1]], axis=1)
    nxt = jnp.concatenate([x[:, 1:], zero], axis=1)
    return 0.5 * (prev + nxt)


def _rwkv_scan(r, w, k, v, kk, a, s0):
    def step(S, inp):
        rt, wt, kt, vt, kkt, at = inp
        sa = jnp.einsum("bhvk,bhk->bhv", S, kkt)
        S = S * wt[:, :, None, :] - sa[..., None] * (kkt * at)[:, :, None, :] + vt[..., None] * kt[:, :, None, :]
        return S, jnp.einsum("bhvk,bhk->bhv", S, rt)
    xs = tuple(jnp.moveaxis(t, 1, 0) for t in (r, w, k, v, kk, a))
    S, ys = lax.scan(step, s0, xs)
    return jnp.moveaxis(ys, 0, 1), S


def rwkv_mixer(z, s0, mu, w0, w2, a0, a2, g2, k_k, k_a, r_k, gn_w, gn_b):
    bsz, L, _ = z.shape
    z32 = z.astype(F32)
    z32 = z32 + mu.astype(F32) * (_shift_centred(z32) - z32)
    o = [MIX_W, 2 * MIX_W, 3 * MIX_W, 3 * MIX_W + 2 * RW_LORA_W, 3 * MIX_W + 2 * RW_LORA_W + 2 * RW_LORA_A]
    r, k, v, wl, al, gl = jnp.split(z32, o, axis=-1)
    hs = lambda t: t.reshape(bsz, L, RW_HEADS, RW_N)
    g = jax.nn.sigmoid(gl) @ g2.astype(F32)
    kk = hs(k * k_k.astype(F32))
    kk = kk * lax.rsqrt(jnp.sum(kk * kk, axis=-1, keepdims=True) + 1e-12)
    wl = wl.reshape(bsz, L, 2, RW_LORA_W)
    al = al.reshape(bsz, L, 2, RW_LORA_A)
    rh, vh = hs(r), hs(v)
    y = jnp.zeros_like(rh)
    bonus = jnp.zeros_like(rh)
    finals = []
    for dr in range(2):
        wlog = -jax.nn.softplus(-(w0[dr].astype(F32) + jnp.tanh(wl[:, :, dr]) @ w2[dr].astype(F32))) - 0.5
        decay = jnp.exp(-jnp.exp(wlog))
        a = jax.nn.sigmoid(a0[dr].astype(F32) + al[:, :, dr] @ a2[dr].astype(F32))
        kd = hs(k * (1.0 + (a - 1.0) * k_a.astype(F32)))
        seqs = [rh, hs(decay), kd, vh, kk, hs(a)]
        if dr == 1:
            seqs = [jnp.flip(t, axis=1) for t in seqs]
        init = jnp.zeros((bsz, RW_HEADS, RW_N, RW_N), F32) if s0 is None else s0[:, dr].astype(F32)
        yd, s_fin = _rwkv_scan(*seqs, init)
        y = y + (yd if dr == 0 else jnp.flip(yd, axis=1))
        bonus = bonus + jnp.sum(rh * kd * r_k.astype(F32), axis=-1, keepdims=True) * vh
        if s0 is None:
            finals.append(s_fin)
    mean = jnp.mean(y, axis=-1, keepdims=True)
    var = jnp.mean(jnp.square(y - mean), axis=-1, keepdims=True)
    yn = ((y - mean) * lax.rsqrt(var + GN_EPS)).reshape(bsz, L, MIX_W) * gn_w.astype(F32) + gn_b.astype(F32)
    out = (yn + bonus.reshape(bsz, L, MIX_W)) * g
    state = jnp.stack(finals, axis=1) if s0 is None else None
    return out.astype(z.dtype), state


def dense_attn(q, k, v, sink):
    bsz, Lq, KV, G, Dh = q.shape
    nb = Lq // ATT_BLOCK
    scale = Dh ** -0.5
    qb = jnp.moveaxis(q.reshape(bsz, nb, ATT_BLOCK, KV, G, Dh), 1, 0)

    def one_block(q_blk):
        s = jnp.einsum("bqkgd,bskd->bkgqs", q_blk, k).astype(F32) * scale
        p = jax.nn.softmax(s, axis=-1) if sink is None else _softmax_sink(s, sink)
        return jnp.einsum("bkgqs,bskd->bqkgd", p.astype(v.dtype), v)

    out = lax.map(one_block, qb)
    return jnp.moveaxis(out, 0, 1).reshape(bsz, Lq, KV * G * Dh)


def na_latent(q, k, v, k_ctx, v_ctx, rpb):
    bsz, L, H, Dh = q.shape
    rows = L // GRID_W
    kh = min(NA_KH, rows)
    scale = Dh ** -0.5
    qg = q.reshape(bsz, rows, GRID_W, H, Dh)
    kg = k.reshape(bsz, rows, GRID_W, H, Dh)
    vg = v.reshape(bsz, rows, GRID_W, H, Dh)
    cols = np.arange(GRID_W)
    col_start = np.clip(cols - NA_KW // 2, 0, GRID_W - NA_KW)
    col_idx = col_start[:, None] + np.arange(NA_KW)
    col_bias_idx = col_idx - cols[:, None] + NA_KW - 1
    rpb32 = rpb.astype(F32)
    n_nb = kh * NA_KW

    def one_row(args):
        r, q_r = args
        rs = jnp.clip(r - kh // 2, 0, rows - kh)
        k_win = lax.dynamic_slice_in_dim(kg, rs, kh, axis=1)[:, :, col_idx]
        v_win = lax.dynamic_slice_in_dim(vg, rs, kh, axis=1)[:, :, col_idx]
        row_bias_idx = rs + jnp.arange(kh) - r + NA_KH - 1
        bias = rpb32[:, row_bias_idx[None, :, None], col_bias_idx[:, None, :]]
        s_nb = jnp.einsum("bqhd,biqjhd->bhqij", q_r, k_win).astype(F32) * scale + bias[None]
        s_ctx = jnp.einsum("bqhd,bkhd->bhqk", q_r, k_ctx).astype(F32) * scale
        s = jnp.concatenate([s_nb.reshape(bsz, H, GRID_W, n_nb), s_ctx], axis=-1)
        p = jax.nn.softmax(s, axis=-1).astype(q.dtype)
        p_nb = p[..., :n_nb].reshape(bsz, H, GRID_W, kh, NA_KW)
        return (jnp.einsum("bhqij,biqjhd->bqhd", p_nb, v_win)
                + jnp.einsum("bhqk,bkhd->bqhd", p[..., n_nb:], v_ctx))

    out = lax.map(one_row, (jnp.arange(rows), jnp.moveaxis(qg, 1, 0)))
    return jnp.moveaxis(out, 0, 1).reshape(bsz, L, H * Dh)


def swa_latent(q, k, v, k_ctx, v_ctx, sink):
    bsz, L, KV, G, Dh = q.shape
    nb = L // ATT_BLOCK
    scale = Dh ** -0.5
    pad = ((0, 0), (ATT_BLOCK, ATT_BLOCK), (0, 0), (0, 0))
    kp, vp = jnp.pad(k, pad), jnp.pad(v, pad)
    qb = jnp.moveaxis(q.reshape(bsz, nb, ATT_BLOCK, KV, G, Dh), 1, 0)
    rel = np.arange(ATT_BLOCK)[:, None] - (np.arange(3 * ATT_BLOCK)[None, :] - ATT_BLOCK)
    band = np.abs(rel) <= SWA_WINDOW
    n_loc = 3 * ATT_BLOCK

    def one_block(args):
        j, q_blk = args
        k_blk = lax.dynamic_slice_in_dim(kp, j * ATT_BLOCK, n_loc, axis=1)
        v_blk = lax.dynamic_slice_in_dim(vp, j * ATT_BLOCK, n_loc, axis=1)
        kpos = j * ATT_BLOCK - ATT_BLOCK + jnp.arange(n_loc)
        valid = band & ((kpos >= 0) & (kpos < L))[None, :]
        s_loc = jnp.einsum("bqkgd,bskd->bkgqs", q_blk, k_blk).astype(F32) * scale
        s_loc = jnp.where(valid, s_loc, NEG_INF)
        s_ctx = jnp.einsum("bqkgd,bskd->bkgqs", q_blk, k_ctx).astype(F32) * scale
        p = _softmax_sink(jnp.concatenate([s_loc, s_ctx], axis=-1), sink).astype(v.dtype)
        return (jnp.einsum("bkgqs,bskd->bqkgd", p[..., :n_loc], v_blk)
                + jnp.einsum("bkgqs,bskd->bqkgd", p[..., n_loc:], v_ctx))

    out = lax.map(one_block, (jnp.arange(nb), qb))
    return jnp.moveaxis(out, 0, 1).reshape(bsz, L, KV * G * Dh)


def token_mixers(h, P, ctx):
    bsz, L, _ = h.shape
    proj = h @ P["w_in"]
    o1 = MIX_W
    o2 = o1 + RW_IN
    o3 = o2 + NA_IN
    o4 = o3 + SWA_IN
    u_s5, z_rw, z_na, z_swa, gate_pre = jnp.split(proj, [o1, o2, o3, o4], axis=-1)
    is_ctx = ctx is None
    y_s5, st_s5 = s5_mixer(u_s5, None if is_ctx else ctx["s5"], P["s5_lambda_re"], P["s5_lambda_im"],
                           P["s5_log_dt"], P["s5_b_re"], P["s5_b_im"], P["s5_c_re"], P["s5_c_im"],
                           P["s5_d"], P["s5_glu_w"], P["s5_glu_b"])
    y_rw, st_rw = rwkv_mixer(z_rw, None if is_ctx else ctx["rwkv"], P["rwkv_mu"], P["rwkv_w0"], P["rwkv_w2"],
                             P["rwkv_a0"], P["rwkv_a2"], P["rwkv_g2"], P["rwkv_k_k"], P["rwkv_k_a"],
                             P["rwkv_r_k"], P["rwkv_gn_w"], P["rwkv_gn_b"])
    na_q, na_k, na_v = [t.reshape(bsz, L, NA_HEADS, HEAD_DIM) for t in jnp.split(z_na, 3, axis=-1)]
    sq, sk, sv = jnp.split(z_swa, [SWA_HEADS * HEAD_DIM, (SWA_HEADS + SWA_KV_HEADS) * HEAD_DIM], axis=-1)
    sq = sq.reshape(bsz, L, SWA_KV_HEADS, SWA_GROUP, HEAD_DIM)
    sk = sk.reshape(bsz, L, SWA_KV_HEADS, HEAD_DIM)
    sv = sv.reshape(bsz, L, SWA_KV_HEADS, HEAD_DIM)
    if is_ctx:
        y_na = dense_attn(na_q[:, :, :, None, :], na_k, na_v, None)
        y_swa = dense_attn(sq, sk, sv, P["swa_sink"])
        ctx_out = (na_k, na_v, sk, sv, st_s5, st_rw)
    else:
        y_na = na_latent(na_q, na_k, na_v, ctx["na_k"], ctx["na_v"], P["na_rpb"])
        y_swa = swa_latent(rope_2d(sq), rope_2d(sk), sv, ctx["swa_k"], ctx["swa_v"], P["swa_sink"])
        ctx_out = None
    gates = jax.nn.sigmoid(gate_pre.astype(F32)).astype(h.dtype).reshape(bsz, L, N_BRANCH, D_MODEL)
    merged = jnp.zeros_like(h)
    for i, yb in enumerate((y_s5, y_rw, y_na, y_swa)):
        merged = merged + gates[:, :, i] * (yb @ P["w_branch"][i])
    return merged @ P["w_o"], ctx_out


def moe_ffn(h, router_w, router_b, w1, b1, w2, b2):
    bsz, L, D = h.shape
    T = bsz * L
    n_assign = T * TOP_K
    n_blocks = -(-n_assign // MOE_BLOCK) + N_EXPERTS
    xf = h.reshape(T, D)
    logits = (xf @ router_w).astype(F32) + router_b.astype(F32)
    top_v, top_i = lax.top_k(logits, TOP_K)
    gate = jax.nn.softmax(top_v, axis=-1)
    e_flat = top_i.reshape(-1)
    order = jnp.argsort(e_flat)
    e_s = e_flat[order]
    tok_s = (order // TOP_K).astype(jnp.int32)
    g_s = gate.reshape(-1)[order]
    counts = jnp.bincount(e_flat, length=N_EXPERTS)
    padded = (counts + MOE_BLOCK - 1) // MOE_BLOCK * MOE_BLOCK
    start = jnp.cumsum(counts) - counts
    pend = jnp.cumsum(padded)
    pstart = pend - padded
    dest = pstart[e_s] + jnp.arange(n_assign) - start[e_s]
    tok_pad = jnp.full((n_blocks * MOE_BLOCK,), T, jnp.int32).at[dest].set(tok_s)
    g_pad = jnp.zeros((n_blocks * MOE_BLOCK,), F32).at[dest].set(g_s)
    blk_e = jnp.minimum(jnp.searchsorted(pend, jnp.arange(n_blocks) * MOE_BLOCK, side="right"), N_EXPERTS - 1)
    x_ext = jnp.concatenate([xf, jnp.zeros((1, D), xf.dtype)], axis=0)

    def block(acc, inp):
        tok_b, g_b, e = inp
        hb = x_ext[tok_b] @ w1[e] + b1[e]
        glu, lin = jnp.split(hb, 2, axis=-1)
        glu = jnp.minimum(glu, SWIGLU_LIMIT)
        lin = jnp.clip(lin, -SWIGLU_LIMIT, SWIGLU_LIMIT)
        act = glu * jax.nn.sigmoid(SWIGLU_ALPHA * glu) * (lin + 1.0)
        yb = act @ w2[e] + b2[e]
        return acc.at[tok_b].add((yb.astype(F32) * g_b[:, None]).astype(acc.dtype)), None

    acc0 = jnp.zeros((T + 1, D), h.dtype)
    acc, _ = lax.scan(block, acc0, (tok_pad.reshape(n_blocks, MOE_BLOCK), g_pad.reshape(n_blocks, MOE_BLOCK), blk_e))
    return acc[:T].reshape(bsz, L, D)


def trunk_layer(x, cond, P, ctx):
    mod = jax.nn.silu(cond) @ P["ada_w"] + P["ada_b"]
    shift1, scale1, gate1, shift2, scale2, gate2 = jnp.split(mod[:, None, :].astype(x.dtype), 6, axis=-1)
    h = rms_norm(x, P["norms"][0]) * (1.0 + scale1) + shift1
    mix, ctx_out = token_mixers(h, P, ctx)
    x = x + gate1 * rms_norm(mix, P["norms"][1])
    h = rms_norm(x, P["norms"][2]) * (1.0 + scale2) + shift2
    ff = moe_ffn(h, P["router_w"], P["router_b"], P["moe_w1"], P["moe_b1"], P["moe_w2"], P["moe_b2"])
    x = x + gate2 * rms_norm(ff, P["norms"][3])
    return x, ctx_out


def setup_inputs(seed: int = 0) -> dict:
    key = jax.random.key(seed)
    ks = jax.random.split(key, 48)
    nrm = lambda i, shape, s: jax.random.normal(ks[i], shape, F32) * s
    unif = lambda i, shape, lo, hi: jax.random.uniform(ks[i], shape, F32, lo, hi)
    G, P_, C = S5_GROUPS, S5_P, S5_CH
    E, F = N_EXPERTS, EXPERT_FF
    lam_re = -0.5 + nrm(14, (DEPTH, 2, G, P_), 0.005)
    lam_im = jnp.pi * jnp.arange(P_, dtype=F32) + nrm(15, (DEPTH, 2, G, P_), 0.01)
    return {
        "x_prompt": nrm(0, (BATCH, SEQ, D_MODEL), 1.0),
        "x_sample": nrm(1, (DEC_BATCH, DEC_SEQ, D_MODEL), 1.0),
        "cache_na_k": nrm(2, (DEC_BATCH, DEPTH, PAST_LEN, NA_HEADS, HEAD_DIM), 1.0),
        "cache_na_v": nrm(3, (DEC_BATCH, DEPTH, PAST_LEN, NA_HEADS, HEAD_DIM), 1.0),
        "cache_swa_k": nrm(4, (DEC_BATCH, DEPTH, PAST_LEN, SWA_KV_HEADS, HEAD_DIM), 1.0),
        "cache_swa_v": nrm(5, (DEC_BATCH, DEPTH, PAST_LEN, SWA_KV_HEADS, HEAD_DIM), 1.0),
        "state_s5": nrm(6, (DEC_BATCH, DEPTH, 2, G, P_, 2), 0.1),
        "state_rwkv": nrm(7, (DEC_BATCH, DEPTH, 2, RW_HEADS, RW_N, RW_N), 0.3),
        "c": nrm(8, (DEC_BATCH, D_MODEL), 1.0),
        "c_ctx": nrm(9, (D_MODEL,), 1.0),
        "ada_w": nrm(10, (DEPTH, D_MODEL, 6 * D_MODEL), 0.5 * D_MODEL ** -0.5),
        "ada_b": nrm(11, (DEPTH, 6 * D_MODEL), 0.02),
        "norms": 1.0 + nrm(12, (DEPTH, 4, D_MODEL), 0.02),
        "w_in": nrm(13, (DEPTH, D_MODEL, N_IN), D_MODEL ** -0.5),
        "s5_lambda_re": lam_re,
        "s5_lambda_im": lam_im,
        "s5_log_dt": unif(16, (DEPTH, 2, G), math.log(1e-3), math.log(1e-1)),
        "s5_b_re": nrm(17, (DEPTH, 2, G, P_, C), (2 * C) ** -0.5),
        "s5_b_im": nrm(18, (DEPTH, 2, G, P_, C), (2 * C) ** -0.5),
        "s5_c_re": nrm(19, (DEPTH, 2, G, C, P_), P_ ** -0.5),
        "s5_c_im": nrm(20, (DEPTH, 2, G, C, P_), P_ ** -0.5),
        "s5_d": nrm(21, (DEPTH, MIX_W), 1.0),
        "s5_glu_w": nrm(22, (DEPTH, MIX_W, MIX_W), MIX_W ** -0.5),
        "s5_glu_b": nrm(23, (DEPTH, MIX_W), 0.02),
        "rwkv_mu": unif(24, (DEPTH, RW_IN), 0.0, 1.0),
        "rwkv_w0": unif(25, (DEPTH, 2, MIX_W), -6.0, 1.0),
        "rwkv_w2": nrm(26, (DEPTH, 2, RW_LORA_W, MIX_W), 0.5 * RW_LORA_W ** -0.5),
        "rwkv_a0": nrm(27, (DEPTH, 2, MIX_W), 0.5),
        "rwkv_a2": nrm(28, (DEPTH, 2, RW_LORA_A, MIX_W), 0.5 * RW_LORA_A ** -0.5),
        "rwkv_g2": nrm(29, (DEPTH, RW_LORA_G, MIX_W), RW_LORA_G ** -0.5),
        "rwkv_k_k": 0.85 + nrm(30, (DEPTH, MIX_W), 0.05),
        "rwkv_k_a": 1.0 + nrm(31, (DEPTH, MIX_W), 0.05),
        "rwkv_r_k": nrm(32, (DEPTH, RW_HEADS, RW_N), 0.1),
        "rwkv_gn_w": 1.0 + nrm(33, (DEPTH, MIX_W), 0.02),
        "rwkv_gn_b": nrm(34, (DEPTH, MIX_W), 0.02),
        "na_rpb": nrm(35, (DEPTH, NA_HEADS, 2 * NA_KH - 1, 2 * NA_KW - 1), 0.1),
        "swa_sink": nrm(36, (DEPTH, SWA_HEADS), 1.0),
        "w_branch": nrm(37, (DEPTH, N_BRANCH, MIX_W, D_MODEL), MIX_W ** -0.5),
        "w_o": nrm(38, (DEPTH, D_MODEL, D_MODEL), D_MODEL ** -0.5),
        "router_w": nrm(39, (DEPTH, D_MODEL, E), D_MODEL ** -0.5),
        "router_b": nrm(40, (DEPTH, E), 0.01),
        "moe_w1": nrm(41, (DEPTH, E, D_MODEL, 2 * F), D_MODEL ** -0.5),
        "moe_b1": nrm(42, (DEPTH, E, 2 * F), 0.01),
        "moe_w2": nrm(43, (DEPTH, E, F, D_MODEL), F ** -0.5),
        "moe_b2": nrm(44, (DEPTH, E, D_MODEL), 0.01),
    }


def reference(x_prompt, x_sample, cache_na_k, cache_na_v, cache_swa_k, cache_swa_v, state_s5, state_rwkv,
              c, c_ctx, ada_w, ada_b, norms, w_in, s5_lambda_re, s5_lambda_im, s5_log_dt, s5_b_re, s5_b_im,
              s5_c_re, s5_c_im, s5_d, s5_glu_w, s5_glu_b, rwkv_mu, rwkv_w0, rwkv_w2, rwkv_a0, rwkv_a2,
              rwkv_g2, rwkv_k_k, rwkv_k_a, rwkv_r_k, rwkv_gn_w, rwkv_gn_b, na_rpb, swa_sink, w_branch, w_o,
              router_w, router_b, moe_w1, moe_b1, moe_w2, moe_b2):
    xp, xs = x_prompt, x_sample
    nak, nav, swk, swv, s5s, rws = [], [], [], [], [], []
    for l in range(DEPTH):
        P = dict(ada_w=ada_w[l], ada_b=ada_b[l], norms=norms[l], w_in=w_in[l],
                 s5_lambda_re=s5_lambda_re[l], s5_lambda_im=s5_lambda_im[l], s5_log_dt=s5_log_dt[l],
                 s5_b_re=s5_b_re[l], s5_b_im=s5_b_im[l], s5_c_re=s5_c_re[l], s5_c_im=s5_c_im[l],
                 s5_d=s5_d[l], s5_glu_w=s5_glu_w[l], s5_glu_b=s5_glu_b[l],
                 rwkv_mu=rwkv_mu[l], rwkv_w0=rwkv_w0[l], rwkv_w2=rwkv_w2[l], rwkv_a0=rwkv_a0[l],
                 rwkv_a2=rwkv_a2[l], rwkv_g2=rwkv_g2[l], rwkv_k_k=rwkv_k_k[l], rwkv_k_a=rwkv_k_a[l],
                 rwkv_r_k=rwkv_r_k[l], rwkv_gn_w=rwkv_gn_w[l], rwkv_gn_b=rwkv_gn_b[l],
                 na_rpb=na_rpb[l], swa_sink=swa_sink[l], w_branch=w_branch[l], w_o=w_o[l],
                 router_w=router_w[l], router_b=router_b[l], moe_w1=moe_w1[l], moe_b1=moe_b1[l],
                 moe_w2=moe_w2[l], moe_b2=moe_b2[l])
        xp, (k_na, v_na, k_sw, v_sw, st5, strw) = trunk_layer(xp, c_ctx[None, :], P, None)
        nak.append(k_na)
        nav.append(v_na)
        swk.append(k_sw)
        swv.append(v_sw)
        s5s.append(st5)
        rws.append(strw)
        ctx = dict(na_k=cache_na_k[:, l], na_v=cache_na_v[:, l], swa_k=cache_swa_k[:, l],
                   swa_v=cache_swa_v[:, l], s5=state_s5[:, l], rwkv=state_rwkv[:, l])
        xs, _ = trunk_layer(xs, c, P, ctx)
    new_na_k = jnp.stack(nak, axis=1)
    new_na_v = jnp.stack(nav, axis=1)
    new_swa_k = jnp.stack(swk, axis=1)
    new_swa_v = jnp.stack(swv, axis=1)
    new_s5 = jnp.stack(s5s, axis=1)
    new_rwkv = jnp.stack(rws, axis=1)
    return (xp, xs, new_na_k, new_na_v, new_swa_k, new_swa_v, new_s5, new_rwkv)
```

```python
import functools
import math

import numpy as np
import jax
import jax.numpy as jnp
from jax import lax
from jax.experimental import pallas as pl
from jax.experimental.pallas import tpu as pltpu

F32 = jnp.float32
BF16 = jnp.bfloat16

HEAD_DIM = 64
GRID_W = 64
NA_KH = 8
NA_KW = 16
ATT_BLOCK = 128
SWA_WINDOW = 128
ROPE_BASE = 10000.0
NEG_INF = -1e30
TOP_K = 4
SWIGLU_LIMIT = 7.0
SWIGLU_ALPHA = 1.702
RMS_EPS = 1e-6
GN_EPS = 64e-5
LANES = 128
VMEM_BUDGET = 56 * 1024 * 1024


def _cp(sem, vmem=VMEM_BUDGET):
    return pltpu.CompilerParams(dimension_semantics=sem, vmem_limit_bytes=vmem)


def _div_tile(n, pref, quantum):
    if n <= pref:
        return n
    t = (pref // quantum) * quantum
    while t > quantum and n % t:
        t -= quantum
    assert n % t == 0, (n, pref, quantum)
    return t


def _mm_kernel(a_ref, b_ref, *rest, has_bias, act, nk):
    if has_bias:
        bias_ref, rest = rest[0], rest[1:]
    o_ref = rest[0]
    part = jnp.dot(a_ref[...].astype(BF16), b_ref[...].astype(BF16), preferred_element_type=F32)

    def finish(acc):
        if has_bias:
            acc = acc + bias_ref[...]
        if act == "sigmoid":
            acc = jax.nn.sigmoid(acc)
        o_ref[...] = acc.astype(o_ref.dtype)

    if nk == 1:
        finish(part)
    else:
        acc_ref = rest[1]
        k = pl.program_id(2)

        @pl.when(k == 0)
        def _():
            acc_ref[...] = part

        @pl.when(k > 0)
        def _():
            acc_ref[...] += part

        @pl.when(k == nk - 1)
        def _():
            finish(acc_ref[...])


def _matmul(a, b, bias=None, *, out_dtype=F32, tm=1024, tn=512, tk=None, act=None):
    m, kdim = a.shape
    n = b.shape[1]
    tm = _div_tile(m, tm, 8)
    tn = _div_tile(n, tn, LANES)
    tk = kdim if tk is None else _div_tile(kdim, tk, LANES)
    nk = kdim // tk
    in_specs = [pl.BlockSpec((tm, tk), lambda i, j, k: (i, k)),
                pl.BlockSpec((tk, tn), lambda i, j, k: (k, j))]
    args = [a, b]
    if bias is not None:
        in_specs.append(pl.BlockSpec((1, tn), lambda i, j, k: (0, j)))
        args.append(bias.reshape(1, n).astype(F32))
    scratch = [pltpu.VMEM((tm, tn), F32)] if nk > 1 else []
    return pl.pallas_call(
        functools.partial(_mm_kernel, has_bias=bias is not None, act=act, nk=nk),
        grid=(m // tm, n // tn, nk),
        in_specs=in_specs,
        out_specs=pl.BlockSpec((tm, tn), lambda i, j, k: (i, j)),
        out_shape=jax.ShapeDtypeStruct((m, n), out_dtype),
        scratch_shapes=scratch,
        compiler_params=_cp(("parallel", "parallel", "arbitrary")),
    )(*args)


def _ada_kernel(c_ref, w_ref, b_ref, o_ref):
    c = c_ref[...]
    s = c * jax.nn.sigmoid(c)
    o_ref[...] = jnp.dot(s.astype(BF16), w_ref[...].astype(BF16), preferred_element_type=F32) + b_ref[...]


def _ada_mod(cond, ada_w, ada_b):
    r, d = cond.shape
    n = ada_w.shape[1]
    tn = _div_tile(n, 512, LANES)
    return pl.pallas_call(
        _ada_kernel,
        grid=(n // tn,),
        in_specs=[pl.BlockSpec((r, d), lambda j: (0, 0)),
                  pl.BlockSpec((d, tn), lambda j: (0, j)),
                  pl.BlockSpec((1, tn), lambda j: (0, j))],
        out_specs=pl.BlockSpec((r, tn), lambda j: (0, j)),
        out_shape=jax.ShapeDtypeStruct((r, n), F32),
        compiler_params=_cp(("parallel",)),
    )(cond, ada_w, ada_b.reshape(1, n))


class _Tokens:
    def __init__(self, bc, lc, bl, ll):
        self.bc, self.lc, self.bl, self.ll = bc, lc, bl, ll
        self.tc = bc * lc
        self.tl = bl * ll
        self.t = self.tc + self.tl
        self.tm = min(1024, math.gcd(self.tc, ll))

    def mod_index(self, tm):
        nbc = self.tc // tm
        nbl = self.ll // tm
        return lambda i: (jnp.where(i < nbc, 0, 1 + (i - nbc) // nbl), 0, 0)


def _normmod_kernel(x_ref, g_ref, mod_ref, o_ref, *, shift_row, scale_row):
    x = x_ref[...]
    y = x * lax.rsqrt(jnp.mean(x * x, axis=-1, keepdims=True) + RMS_EPS) * g_ref[...]
    scale = mod_ref[0, scale_row:scale_row + 1, :]
    shift = mod_ref[0, shift_row:shift_row + 1, :]
    o_ref[...] = (y * (1.0 + scale) + shift).astype(o_ref.dtype)


def _norm_mod(x, g, mod, tok, *, shift_row, scale_row):
    t, d = x.shape
    tm = min(tok.tm, 512)
    midx = tok.mod_index(tm)
    return pl.pallas_call(
        functools.partial(_normmod_kernel, shift_row=shift_row, scale_row=scale_row),
        grid=(t // tm,),
        in_specs=[pl.BlockSpec((tm, d), lambda i: (i, 0)),
                  pl.BlockSpec((1, d), lambda i: (0, 0)),
                  pl.BlockSpec((1, 6, d), midx)],
        out_specs=pl.BlockSpec((tm, d), lambda i: (i, 0)),
        out_shape=jax.ShapeDtypeStruct((t, d), BF16),
        compiler_params=_cp(("parallel",)),
    )(x, g.reshape(1, d), mod)


def _s5_scan_kernel(a_ref, h0_ref, bu_ref, h_ref, fin_ref, st_ref, *, tc, nr):
    d = pl.program_id(1)
    c = pl.program_id(2)

    @pl.when(c == 0)
    def _():
        st_ref[...] = h0_ref[0, 0]

    ar = a_ref[0, :nr]
    ai = a_ref[0, nr:]

    def body(i, carry):
        hr, hi = carry
        t = jnp.where(d == 0, i, tc - 1 - i)
        bu = bu_ref[t]
        nhr = ar * hr - ai * hi + bu[:nr]
        nhi = ar * hi + ai * hr + bu[nr:]
        h_ref[t, :nr] = nhr.astype(h_ref.dtype)
        h_ref[t, nr:] = nhi.astype(h_ref.dtype)
        return nhr, nhi

    hr, hi = lax.fori_loop(0, tc, body, (st_ref[:nr], st_ref[nr:]), unroll=4)
    st_ref[:nr] = hr
    st_ref[nr:] = hi
    fin_ref[0, 0, :nr] = hr
    fin_ref[0, 0, nr:] = hi


def _s5_scan(bu, a, h0, nseq, seqlen):
    t, four_nr, _ = bu.shape
    nr = four_nr // 4
    tc = min(256, seqlen)
    nch = seqlen // tc

    def tidx(s, d, c):
        return (s * nch + jnp.where(d == 0, c, nch - 1 - c), d, 0)

    return pl.pallas_call(
        functools.partial(_s5_scan_kernel, tc=tc, nr=nr),
        grid=(nseq, 2, nch),
        in_specs=[pl.BlockSpec((1, 2 * nr, LANES), lambda s, d, c: (d, 0, 0)),
                  pl.BlockSpec((1, 1, 2 * nr, LANES), lambda s, d, c: (s, d, 0, 0)),
                  pl.BlockSpec((tc, 2 * nr, LANES), tidx)],
        out_specs=[pl.BlockSpec((tc, 2 * nr, LANES), tidx),
                   pl.BlockSpec((1, 1, 2 * nr, LANES), lambda s, d, c: (s, d, 0, 0))],
        out_shape=[jax.ShapeDtypeStruct((t, four_nr, LANES), BF16),
                   jax.ShapeDtypeStruct((nseq, 2, 2 * nr, LANES), F32)],
        scratch_shapes=[pltpu.VMEM((2 * nr, LANES), F32)],
        compiler_params=_cp(("parallel", "parallel", "arbitrary")),
    )(a, h0, bu)


def _s5_out_kernel(yd_ref, u_ref, d_ref, w_ref, b_ref, o_ref):
    y = d_ref[...] * u_ref[...] + yd_ref[...]
    g = jax.nn.gelu(y)
    z = jnp.dot(g.astype(BF16), w_ref[...], preferred_element_type=F32) + b_ref[...]
    o_ref[...] = (g * jax.nn.sigmoid(z)).astype(o_ref.dtype)


def _s5_out(yd, u, d_skip, glu_w, glu_b):
    t, mw = u.shape
    tm = _div_tile(t, 512, 8)
    return pl.pallas_call(
        _s5_out_kernel,
        grid=(t // tm,),
        in_specs=[pl.BlockSpec((tm, mw), lambda i: (i, 0)),
                  pl.BlockSpec((tm, mw), lambda i: (i, 0)),
                  pl.BlockSpec((1, mw), lambda i: (0, 0)),
                  pl.BlockSpec((mw, mw), lambda i: (0, 0)),
                  pl.BlockSpec((1, mw), lambda i: (0, 0))],
        out_specs=pl.BlockSpec((tm, mw), lambda i: (i, 0)),
        out_shape=jax.ShapeDtypeStruct((t, mw), BF16),
        compiler_params=_cp(("parallel",)),
    )(yd, u, d_skip.reshape(1, mw), glu_w.astype(BF16), glu_b.reshape(1, mw))


def _s5_weights(lam_re, lam_im, log_dt, b_re, b_im, c_re, c_im):
    _, g, p = lam_re.shape
    c = b_re.shape[-1]
    dt = jnp.exp(log_dt)[:, :, None]
    mag = jnp.exp(lam_re * dt)
    abr, abi = mag * jnp.cos(lam_im * dt), mag * jnp.sin(lam_im * dt)
    den = lam_re * lam_re + lam_im * lam_im
    ir, ii = lam_re / den, -lam_im / den
    cr = (abr - 1.0) * ir - abi * ii
    ci = (abr - 1.0) * ii + abi * ir
    bbr = cr[..., None] * b_re - ci[..., None] * b_im
    bbi = cr[..., None] * b_im + ci[..., None] * b_re
    eye = jnp.eye(g, dtype=F32)
    wb = jnp.stack([bbr, bbi], axis=1)
    wb = jnp.einsum("dzgpc,gh->gcdzhp", wb, eye).reshape(g * c, 4 * g * p)
    wc = jnp.stack([c_re, -c_im], axis=1)
    wc = jnp.einsum("dzgcp,gh->dzgphc", wc, eye).reshape(4 * g * p, g * c)
    a = jnp.stack([abr.reshape(2, g * p), abi.reshape(2, g * p)], axis=1)
    return wb.astype(BF16), wc.astype(BF16), a.reshape(2, 2 * g * p // LANES, LANES)


def _s5_mixer(u, P, tok, state_s5):
    g, p = P["s5_lambda_re"].shape[1:]
    gp = g * p
    wb, wc, a = _s5_weights(P["s5_lambda_re"], P["s5_lambda_im"], P["s5_log_dt"], P["s5_b_re"], P["s5_b_im"],
                            P["s5_c_re"], P["s5_c_im"])
    bu = _matmul(u, wb, out_dtype=F32, tm=1024, tn=1024)
    bu = bu.reshape(tok.t, 4 * gp // LANES, LANES)
    nr2 = 2 * gp // LANES
    h0_lat = jnp.moveaxis(state_s5, -1, 2).reshape(tok.bl, 2, nr2, LANES)
    h0_ctx = jnp.zeros((tok.bc, 2, nr2, LANES), F32)
    h_ctx, fin_ctx = _s5_scan(bu[:tok.tc], a, h0_ctx, tok.bc, tok.lc)
    h_lat, _ = _s5_scan(bu[tok.tc:], a, h0_lat, tok.bl, tok.ll)
    h = jnp.concatenate([h_ctx, h_lat], axis=0).reshape(tok.t, 4 * gp)
    yd = _matmul(h, wc, out_dtype=F32, tm=1024, tn=1024, tk=2048)
    y = _s5_out(yd, u, P["s5_d"], P["s5_glu_w"], P["s5_glu_b"])
    new_state = jnp.moveaxis(fin_ctx.reshape(tok.bc, 2, 2, g, p), 2, -1)
    return y, new_state


def _softplus(x):
    return jnp.maximum(x, 0.0) + jnp.log(1.0 + jnp.exp(-jnp.abs(x)))


def _rwkv_pre_kernel(z_ref, zp_ref, zn_ref, mu_ref, w0_ref, w2_ref, a0_ref, a2_ref, g2_ref, x_ref, g_ref,
                     *, tm, tc_rows, lc, ll, mw, lw, la):
    i = pl.program_id(0)
    z = z_ref[...]
    loc = lax.broadcasted_iota(jnp.int32, (tm, 1), 0)
    row = i * tm + loc
    in_ctx = row < tc_rows
    pos = jnp.where(in_ctx, lax.rem(row, lc), lax.rem(row - tc_rows, ll))
    last = jnp.where(in_ctx, lc - 1, ll - 1)
    prev = pltpu.roll(z, 1, axis=0)
    prev = jnp.where(loc == 0, zp_ref[7:8, :], prev)
    prev = jnp.where(pos == 0, 0.0, prev)
    nxt = pltpu.roll(z, tm - 1, axis=0)
    nxt = jnp.where(loc == tm - 1, zn_ref[0:1, :], nxt)
    nxt = jnp.where(pos == last, 0.0, nxt)
    zs = z + mu_ref[...] * (0.5 * (prev + nxt) - z)
    o3 = 3 * mw
    k = zs[:, mw:2 * mw]
    x_ref[0] = zs[:, :mw]
    x_ref[1] = k
    x_ref[2] = zs[:, 2 * mw:o3]
    wl = zs[:, o3:o3 + 2 * lw]
    al = zs[:, o3 + 2 * lw:o3 + 2 * lw + 2 * la]
    gl = zs[:, o3 + 2 * lw + 2 * la:]
    g_ref[...] = jnp.dot(jax.nn.sigmoid(gl).astype(BF16), g2_ref[...], preferred_element_type=F32)
    wlora = jnp.dot(jnp.tanh(wl).astype(BF16), w2_ref[...], preferred_element_type=F32)
    alora = jnp.dot(al.astype(BF16), a2_ref[...], preferred_element_type=F32)
    for dr in range(2):
        wlog = -_softplus(-(w0_ref[dr:dr + 1, :] + wlora[:, dr * mw:(dr + 1) * mw])) - 0.5
        x_ref[3 + dr] = jnp.exp(-jnp.exp(wlog))
        x_ref[5 + dr] = jax.nn.sigmoid(a0_ref[dr:dr + 1, :] + alora[:, dr * mw:(dr + 1) * mw])


def _blockdiag2(w):
    _, r, c = w.shape
    z = jnp.zeros((r, c), w.dtype)
    return jnp.concatenate([jnp.concatenate([w[0], z], axis=1), jnp.concatenate([z, w[1]], axis=1)], axis=0)


def _rwkv_pre(z, P, tok):
    t, rw_in = z.shape
    mw = P["rwkv_k_k"].shape[0]
    lw = P["rwkv_w2"].shape[1]
    la = P["rwkv_a2"].shape[1]
    tm = min(256, math.gcd(tok.tc, tok.ll))
    nb8 = tm // 8
    last8 = t // 8 - 1
    return pl.pallas_call(
        functools.partial(_rwkv_pre_kernel, tm=tm, tc_rows=tok.tc, lc=tok.lc, ll=tok.ll, mw=mw, lw=lw, la=la),
        grid=(t // tm,),
        in_specs=[pl.BlockSpec((tm, rw_in), lambda i: (i, 0)),
                  pl.BlockSpec((8, rw_in), lambda i: (jnp.maximum(i * nb8 - 1, 0), 0)),
                  pl.BlockSpec((8, rw_in), lambda i: (jnp.minimum((i + 1) * nb8, last8), 0)),
                  pl.BlockSpec((1, rw_in), lambda i: (0, 0)),
                  pl.BlockSpec((2, mw), lambda i: (0, 0)),
                  pl.BlockSpec((2 * lw, 2 * mw), lambda i: (0, 0)),
                  pl.BlockSpec((2, mw), lambda i: (0, 0)),
                  pl.BlockSpec((2 * la, 2 * mw), lambda i: (0, 0)),
                  pl.BlockSpec((P["rwkv_g2"].shape[0], mw), lambda i: (0, 0))],
        out_specs=[pl.BlockSpec((7, tm, mw), lambda i: (0, i, 0)),
                   pl.BlockSpec((tm, mw), lambda i: (i, 0))],
        out_shape=[jax.ShapeDtypeStruct((7, t, mw), F32), jax.ShapeDtypeStruct((t, mw), F32)],
        compiler_params=_cp(("parallel",)),
    )(z, z, z, P["rwkv_mu"].reshape(1, rw_in), P["rwkv_w0"], _blockdiag2(P["rwkv_w2"]).astype(BF16),
      P["rwkv_a0"], _blockdiag2(P["rwkv_a2"]).astype(BF16), P["rwkv_g2"].astype(BF16))


def _rwkv_scan_kernel(r_ref, k_ref, v_ref, w_ref, a_ref, par_ref, s0_ref, y_ref, sf_ref, s_ref, *, tc, n, nch):
    d = pl.program_id(1)
    c = pl.program_id(2)

    @pl.when(c == 0)
    def _():
        s_ref[...] = s0_ref[0]

    kkp = par_ref[0]
    kap = par_ref[1]

    def step(i, carry):
        t = jnp.where(d == 0, i, tc - 1 - i)
        r = r_ref[0, t]
        k = k_ref[0, t]
        v = v_ref[0, t]
        w = w_ref[0, t]
        a = a_ref[0, t]
        kk = k * kkp
        kk = kk * lax.rsqrt(jnp.sum(kk * kk, axis=0, keepdims=True) + 1e-12)
        kd = k * (1.0 + (a - 1.0) * kap)
        kka = kk * a
        for vi in range(n):
            s = s_ref[vi]
            sa = jnp.sum(s * kk, axis=0, keepdims=True)
            s = s * w - sa * kka + v[vi:vi + 1, :] * kd
            s_ref[vi] = s
            y_ref[0, t, vi:vi + 1, :] = jnp.sum(s * r, axis=0, keepdims=True)
        return carry

    lax.fori_loop(0, tc, step, 0)

    @pl.when(c == nch - 1)
    def _():
        sf_ref[0] = s_ref[...]


def _rwkv_scan(x, par, s0, seqlen):
    _, l, n, ni = x.shape
    nl = min(LANES, ni)
    tc = min(32, seqlen)
    nch = seqlen // tc

    def tix(j):
        return lambda g, d, c: (j(d), jnp.where(d == 0, c, nch - 1 - c), 0, g)

    xspec = lambda j: pl.BlockSpec((1, tc, n, nl), tix(j))
    return pl.pallas_call(
        functools.partial(_rwkv_scan_kernel, tc=tc, n=n, nch=nch),
        grid=(ni // nl, 2, nch),
        in_specs=[xspec(lambda d: 0), xspec(lambda d: 1), xspec(lambda d: 2),
                  xspec(lambda d: 3 + d), xspec(lambda d: 5 + d),
                  pl.BlockSpec((4, n, nl), lambda g, d, c: (0, 0, g)),
                  pl.BlockSpec((1, n, n, nl), lambda g, d, c: (d, 0, 0, g))],
        out_specs=[pl.BlockSpec((1, tc, n, nl), tix(lambda d: d)),
                   pl.BlockSpec((1, n, n, nl), lambda g, d, c: (d, 0, 0, g))],
        out_shape=[jax.ShapeDtypeStruct((2, l, n, ni), F32), jax.ShapeDtypeStruct((2, n, n, ni), F32)],
        scratch_shapes=[pltpu.VMEM((n, n, nl), F32)],
        compiler_params=_cp(("parallel", "parallel", "arbitrary")),
    )(x, x, x, x, x, par, s0)


def _rwkv_post_kernel(y_ref, r_ref, k_ref, v_ref, a0_ref, a1_ref, par_ref, o_ref):
    y = y_ref[0] + y_ref[1]
    mean = jnp.mean(y, axis=1, keepdims=True)
    yc = y - mean
    var = jnp.mean(yc * yc, axis=1, keepdims=True)
    kap, rk, gnw, gnb = par_ref[1], par_ref[2], par_ref[3], par_ref[4]
    yn = yc * lax.rsqrt(var + GN_EPS) * gnw + gnb
    r = r_ref[0]
    k = k_ref[0]
    rkk = r * k * rk
    bsum = (jnp.sum(rkk * (1.0 + (a0_ref[0] - 1.0) * kap), axis=1, keepdims=True)
            + jnp.sum(rkk * (1.0 + (a1_ref[0] - 1.0) * kap), axis=1, keepdims=True))
    o_ref[...] = yn + bsum * v_ref[0]


def _rwkv_post(y, x, par, seqlen):
    _, l, n, ni = x.shape
    nl = min(LANES, ni)
    tc = min(64, seqlen)
    xspec = lambda j: pl.BlockSpec((1, tc, n, nl), lambda g, c: (j, c, 0, g))
    return pl.pallas_call(
        _rwkv_post_kernel,
        grid=(ni // nl, seqlen // tc),
        in_specs=[pl.BlockSpec((2, tc, n, nl), lambda g, c: (0, c, 0, g)),
                  xspec(0), xspec(1), xspec(2), xspec(5), xspec(6),
                  pl.BlockSpec((5, n, nl), lambda g, c: (0, 0, g))],
        out_specs=pl.BlockSpec((tc, n, nl), lambda g, c: (c, 0, g)),
        out_shape=jax.ShapeDtypeStruct((l, n, ni), F32),
        compiler_params=_cp(("parallel", "parallel")),
    )(y, x, x, x, x, x, par)


def _mul_kernel(a_ref, b_ref, o_ref):
    o_ref[...] = (a_ref[...] * b_ref[...]).astype(o_ref.dtype)


def _mul_bf16(a, b):
    t, n = a.shape
    tm = _div_tile(t, 1024, 8)
    spec = pl.BlockSpec((tm, n), lambda i: (i, 0))
    return pl.pallas_call(_mul_kernel, grid=(t // tm,), in_specs=[spec, spec], out_specs=spec,
                          out_shape=jax.ShapeDtypeStruct((t, n), BF16), compiler_params=_cp(("parallel",)))(a, b)


def _rwkv_mixer(z, P, tok, state_rwkv):
    mw = P["rwkv_k_k"].shape[0]
    n = HEAD_DIM
    h = mw // n
    x, g = _rwkv_pre(z, P, tok)

    def tile(p, b):
        return jnp.tile(p.reshape(h, n).T[:, None, :], (1, b, 1)).reshape(n, b * h)

    def run(xg, b, l, s0):
        xi = xg.reshape(7, b, l, h, n).transpose(0, 2, 4, 1, 3).reshape(7, l, n, b * h)
        par = jnp.stack([tile(P["rwkv_k_k"], b), tile(P["rwkv_k_a"], b), tile(P["rwkv_r_k"].reshape(-1), b),
                         tile(P["rwkv_gn_w"], b), tile(P["rwkv_gn_b"], b)])
        y, sf = _rwkv_scan(xi, par[:4], s0, l)
        o = _rwkv_post(y, xi, par, l)
        o = o.reshape(l, n, b, h).transpose(2, 0, 3, 1).reshape(b * l, mw)
        return o, sf

    s0_lat = state_rwkv.transpose(1, 3, 4, 0, 2).reshape(2, n, n, tok.bl * h)
    s0_ctx = jnp.zeros((2, n, n, tok.bc * h), F32)
    o_ctx, sf_ctx = run(x[:, :tok.tc], tok.bc, tok.lc, s0_ctx)
    o_lat, _ = run(x[:, tok.tc:], tok.bl, tok.ll, s0_lat)
    y = _mul_bf16(jnp.concatenate([o_ctx, o_lat], axis=0), g)
    new_state = sf_ctx.reshape(2, n, n, tok.bc, h).transpose(3, 0, 4, 1, 2)
    return y, new_state


def _nt_dot(a, b):
    return lax.dot_general(a, b, (((1,), (1,)), ((), ())), preferred_element_type=F32)


def _dense_attn_kernel(*refs, ng, has_sink, scale):
    if has_sink:
        sink_ref, q_ref, k_ref, v_ref, o_ref = refs
        kv = pl.program_id(1)
    else:
        q_ref, k_ref, v_ref, o_ref = refs
    k = k_ref[0, 0]
    v = v_ref[0, 0]
    for g in range(ng):
        s = _nt_dot(q_ref[0, 0, g], k) * scale
        m = jnp.max(s, axis=-1, keepdims=True)
        if has_sink:
            sk = sink_ref[kv, g]
            m = jnp.maximum(m, sk)
        p = jnp.exp(s - m)
        l = jnp.sum(p, axis=-1, keepdims=True)
        if has_sink:
            l = l + jnp.exp(sk - m)
        o = jnp.dot(p.astype(BF16), v, preferred_element_type=F32)
        o_ref[0, 0, g] = (o / l).astype(o_ref.dtype)


def _dense_attn(q, k, v, sink):
    b, kvh, ng, l, dh = q.shape
    lk = k.shape[2]
    in_specs = [pl.BlockSpec((1, 1, ng, l, dh), lambda i, j: (i, j, 0, 0, 0)),
                pl.BlockSpec((1, 1, lk, dh), lambda i, j: (i, j, 0, 0)),
                pl.BlockSpec((1, 1, lk, dh), lambda i, j: (i, j, 0, 0))]
    args = [q, k, v]
    if sink is not None:
        in_specs = [pl.BlockSpec(memory_space=pltpu.SMEM)] + in_specs
        args = [sink] + args
    return pl.pallas_call(
        functools.partial(_dense_attn_kernel, ng=ng, has_sink=sink is not None, scale=dh ** -0.5),
        grid=(b, kvh),
        in_specs=in_specs,
        out_specs=pl.BlockSpec((1, 1, ng, l, dh), lambda i, j: (i, j, 0, 0, 0)),
        out_shape=jax.ShapeDtypeStruct((b, kvh, ng, l, dh), BF16),
        compiler_params=_cp(("parallel", "parallel")),
    )(*args)


def _na_latent_kernel(q_ref, k_ref, v_ref, kc_ref, vc_ref, bias_ref, o_ref, *, rows, kh, scale):
    kc = kc_ref[0, 0]
    vc = vc_ref[0, 0]
    nwin = kh * GRID_W

    def row(r, carry):
        rs = jnp.clip(r - kh // 2, 0, rows - kh)
        q = q_ref[0, 0, pl.ds(pl.multiple_of(r * GRID_W, GRID_W), GRID_W), :]
        base = pl.multiple_of(rs * GRID_W, GRID_W)
        kw = k_ref[0, 0, pl.ds(base, nwin), :]
        vw = v_ref[0, 0, pl.ds(base, nwin), :]
        s_nb = _nt_dot(q, kw) * scale + bias_ref[0, r - rs]
        s_cx = _nt_dot(q, kc) * scale
        m = jnp.maximum(jnp.max(s_nb, axis=-1, keepdims=True), jnp.max(s_cx, axis=-1, keepdims=True))
        p_nb = jnp.exp(s_nb - m)
        p_cx = jnp.exp(s_cx - m)
        l = jnp.sum(p_nb, axis=-1, keepdims=True) + jnp.sum(p_cx, axis=-1, keepdims=True)
        o = (jnp.dot(p_nb.astype(BF16), vw, preferred_element_type=F32)
             + jnp.dot(p_cx.astype(BF16), vc, preferred_element_type=F32))
        o_ref[0, 0, pl.ds(pl.multiple_of(r * GRID_W, GRID_W), GRID_W), :] = (o / l).astype(o_ref.dtype)
        return carry

    lax.fori_loop(0, rows, row, 0)


def _na_bias_table(rpb, rows):
    kh = min(NA_KH, rows)
    cols = np.arange(GRID_W)
    col_start = np.clip(cols - NA_KW // 2, 0, GRID_W - NA_KW)
    kcol = np.arange(GRID_W)
    inside = (kcol[None, :] >= col_start[:, None]) & (kcol[None, :] < col_start[:, None] + NA_KW)
    cb = np.clip(kcol[None, :] - cols[:, None] + NA_KW - 1, 0, 2 * NA_KW - 2)
    cls = np.arange(kh)
    rb = np.arange(kh)[None, :] - cls[:, None] + NA_KH - 1
    tab = rpb.astype(F32)[:, rb[:, None, :, None], cb[None, :, None, :]]
    tab = jnp.where(inside[None, None, :, None, :], tab, NEG_INF)
    return tab.reshape(rpb.shape[0], kh, GRID_W, kh * GRID_W)


def _na_latent(q, k, v, kc, vc, rpb):
    b, h, l, dh = q.shape
    lc = kc.shape[2]
    rows = l // GRID_W
    kh = min(NA_KH, rows)
    bias = _na_bias_table(rpb, rows)
    seq = pl.BlockSpec((1, 1, l, dh), lambda i, j: (i, j, 0, 0))
    ctx = pl.BlockSpec((1, 1, lc, dh), lambda i, j: (i, j, 0, 0))
    return pl.pallas_call(
        functools.partial(_na_latent_kernel, rows=rows, kh=kh, scale=dh ** -0.5),
        grid=(b, h),
        in_specs=[seq, seq, seq, ctx, ctx,
                  pl.BlockSpec((1, kh, GRID_W, kh * GRID_W), lambda i, j: (j, 0, 0, 0))],
        out_specs=seq,
        out_shape=jax.ShapeDtypeStruct((b, h, l, dh), BF16),
        compiler_params=_cp(("parallel", "parallel")),
    )(q, k, v, kc, vc, bias)


def _swa_latent_kernel(sink_ref, q_ref, kp_ref, k0_ref, kn_ref, vp_ref, v0_ref, vn_ref, kc_ref, vc_ref, o_ref,
                       *, ng, nb, scale):
    kv = pl.program_id(1)
    j = pl.program_id(2)
    qi = lax.broadcasted_iota(jnp.int32, (ATT_BLOCK, ATT_BLOCK), 0)
    ki = lax.broadcasted_iota(jnp.int32, (ATT_BLOCK, ATT_BLOCK), 1)
    ok_prev = (ki >= qi) & (j > 0)
    ok_next = (ki <= qi) & (j < nb - 1)
    kp, k0, kn = kp_ref[0, 0], k0_ref[0, 0], kn_ref[0, 0]
    vp, v0, vn = vp_ref[0, 0], v0_ref[0, 0], vn_ref[0, 0]
    kc, vc = kc_ref[0, 0], vc_ref[0, 0]
    for g in range(ng):
        q = q_ref[0, 0, g]
        sp = jnp.where(ok_prev, _nt_dot(q, kp) * scale, NEG_INF)
        s0 = _nt_dot(q, k0) * scale
        sn = jnp.where(ok_next, _nt_dot(q, kn) * scale, NEG_INF)
        sc = _nt_dot(q, kc) * scale
        sk = sink_ref[kv, g]
        mx = lambda s: jnp.max(s, axis=-1, keepdims=True)
        m = jnp.maximum(jnp.maximum(jnp.maximum(mx(sp), mx(s0)), jnp.maximum(mx(sn), mx(sc))), sk)
        pp, p0, pn, pc = jnp.exp(sp - m), jnp.exp(s0 - m), jnp.exp(sn - m), jnp.exp(sc - m)
        sm = lambda p: jnp.sum(p, axis=-1, keepdims=True)
        l = sm(pp) + sm(p0) + sm(pn) + sm(pc) + jnp.exp(sk - m)
        o = (jnp.dot(pp.astype(BF16), vp, preferred_element_type=F32)
             + jnp.dot(p0.astype(BF16), v0, preferred_element_type=F32)
             + jnp.dot(pn.astype(BF16), vn, preferred_element_type=F32)
             + jnp.dot(pc.astype(BF16), vc, preferred_element_type=F32))
        o_ref[0, 0, g] = (o / l).astype(o_ref.dtype)


def _swa_latent(q, k, v, kc, vc, sink):
    b, kvh, ng, l, dh = q.shape
    lc = kc.shape[2]
    nb = l // ATT_BLOCK
    blk = lambda f: pl.BlockSpec((1, 1, ATT_BLOCK, dh), lambda i, h, j: (i, h, f(j), 0))
    prev, cur, nxt = blk(lambda j: jnp.maximum(j - 1, 0)), blk(lambda j: j), blk(lambda j: jnp.minimum(j + 1, nb - 1))
    ctx = pl.BlockSpec((1, 1, lc, dh), lambda i, h, j: (i, h, 0, 0))
    qspec = pl.BlockSpec((1, 1, ng, ATT_BLOCK, dh), lambda i, h, j: (i, h, 0, j, 0))
    return pl.pallas_call(
        functools.partial(_swa_latent_kernel, ng=ng, nb=nb, scale=dh ** -0.5),
        grid=(b, kvh, nb),
        in_specs=[pl.BlockSpec(memory_space=pltpu.SMEM), qspec, prev, cur, nxt, prev, cur, nxt, ctx, ctx],
        out_specs=qspec,
        out_shape=jax.ShapeDtypeStruct((b, kvh, ng, l, dh), BF16),
        compiler_params=_cp(("parallel", "parallel", "parallel")),
    )(sink, q, k, k, k, v, v, v, kc, vc)


def _rope_kernel(x_ref, cos_ref, sin_ref, o_ref, *, ngroups):
    cos = cos_ref[...]
    sin = sin_ref[...]
    lane = lax.broadcasted_iota(jnp.int32, cos.shape, 1)
    lower = (lane & 31) < 16
    for j in range(ngroups):
        x = x_ref[:, j * LANES:(j + 1) * LANES]
        partner = jnp.where(lower, pltpu.roll(x, LANES - 16, axis=1), pltpu.roll(x, 16, axis=1))
        o_ref[:, j * LANES:(j + 1) * LANES] = (x * cos + partner * sin).astype(o_ref.dtype)


def _rope_tables(l):
    quarter = HEAD_DIM // 4
    t = np.arange(l)
    inv = ROPE_BASE ** (-np.arange(quarter, dtype=np.float32) / quarter)
    ang_r = (t // GRID_W).astype(np.float32)[:, None] * inv
    ang_c = (t % GRID_W).astype(np.float32)[:, None] * inv
    ang = jnp.asarray(np.concatenate([ang_r, ang_r, ang_c, ang_c], axis=1), F32)
    sign = np.tile(np.concatenate([-np.ones(quarter), np.ones(quarter)]), 2).astype(np.float32)
    cos = jnp.tile(jnp.cos(ang), (1, LANES // HEAD_DIM))
    sin = jnp.tile(jnp.sin(ang) * sign, (1, LANES // HEAD_DIM))
    return cos, sin


def _rope(x, seqlen, width):
    t = x.shape[0]
    tm = _div_tile(seqlen, 512, 8)
    nper = seqlen // tm
    cos, sin = _rope_tables(seqlen)
    return pl.pallas_call(
        functools.partial(_rope_kernel, ngroups=width // LANES),
        grid=(t // tm,),
        in_specs=[pl.BlockSpec((tm, width), lambda i: (i, 0)),
                  pl.BlockSpec((tm, LANES), lambda i: (i % nper, 0)),
                  pl.BlockSpec((tm, LANES), lambda i: (i % nper, 0))],
        out_specs=pl.BlockSpec((tm, width), lambda i: (i, 0)),
        out_shape=jax.ShapeDtypeStruct((t, width), BF16),
        compiler_params=_cp(("parallel",)),
    )(x, cos, sin)


def _heads(x, b, l, nh):
    return x.astype(BF16).reshape(b, l, nh, HEAD_DIM).transpose(0, 2, 1, 3)


def _unheads(o):
    b, nh, l, dh = o.shape
    return o.transpose(0, 2, 1, 3).reshape(b * l, nh * dh)


def _attention(z_na, z_swa, P, tok, cache):
    mw = z_na.shape[1] // 3
    h = mw // HEAD_DIM
    kvh = cache["swa_k"].shape[2]
    ng = h // kvh
    sink = P["swa_sink"].astype(F32).reshape(kvh, ng)
    bc, lc, bl, ll, tc = tok.bc, tok.lc, tok.bl, tok.ll, tok.tc
    q, k, v = z_na[:, :mw], z_na[:, mw:2 * mw], z_na[:, 2 * mw:]
    o_ctx = _dense_attn(_heads(q[:tc], bc, lc, h)[:, :, None], _heads(k[:tc], bc, lc, h), _heads(v[:tc], bc, lc, h), None)
    kc = cache["na_k"].astype(BF16).transpose(0, 2, 1, 3)
    vc = cache["na_v"].astype(BF16).transpose(0, 2, 1, 3)
    o_lat = _na_latent(_heads(q[tc:], bl, ll, h), _heads(k[tc:], bl, ll, h), _heads(v[tc:], bl, ll, h), kc, vc, P["na_rpb"])
    y_na = jnp.concatenate([_unheads(o_ctx[:, :, 0]), _unheads(o_lat)], axis=0)
    qw = h * HEAD_DIM
    kw = kvh * HEAD_DIM
    sq, sk, sv = z_swa[:, :qw], z_swa[:, qw:qw + kw], z_swa[:, qw + kw:]
    o_ctx = _dense_attn(_heads(sq[:tc], bc, lc, h).reshape(bc, kvh, ng, lc, HEAD_DIM),
                        _heads(sk[:tc], bc, lc, kvh), _heads(sv[:tc], bc, lc, kvh), sink)
    roped = _rope(z_swa[tc:], ll, qw + kw)
    kc = cache["swa_k"].astype(BF16).transpose(0, 2, 1, 3)
    vc = cache["swa_v"].astype(BF16).transpose(0, 2, 1, 3)
    o_lat = _swa_latent(_heads(roped[:, :qw], bl, ll, h).reshape(bl, kvh, ng, ll, HEAD_DIM),
                        _heads(roped[:, qw:], bl, ll, kvh), _heads(sv[tc:], bl, ll, kvh), kc, vc, sink)
    y_swa = jnp.concatenate([_unheads(o_ctx.reshape(bc, h, lc, HEAD_DIM)),
                             _unheads(o_lat.reshape(bl, h, ll, HEAD_DIM))], axis=0)
    return y_na, y_swa


def _merge_kernel(h_ref, y0_ref, y1_ref, y2_ref, y3_ref, g0_ref, g1_ref, g2_ref, g3_ref,
                  b0_ref, b1_ref, b2_ref, b3_ref, o_ref):
    h = h_ref[...]
    acc = None
    for y_ref, g_ref, b_ref in ((y0_ref, g0_ref, b0_ref), (y1_ref, g1_ref, b1_ref),
                                (y2_ref, g2_ref, b2_ref), (y3_ref, g3_ref, b3_ref)):
        gate = jax.nn.sigmoid(jnp.dot(h, g_ref[...], preferred_element_type=F32))
        term = gate * jnp.dot(y_ref[...], b_ref[0], preferred_element_type=F32)
        acc = term if acc is None else acc + term
    o_ref[...] = acc.astype(o_ref.dtype)


def _merge(h, ys, w_gate, w_branch):
    t, d = h.shape
    mw = ys[0].shape[1]
    tm = _div_tile(t, 512, 8)
    tn = _div_tile(d, 256, LANES)
    nj = d // tn
    rows = lambda w: pl.BlockSpec((tm, w), lambda i, j: (i, 0))
    return pl.pallas_call(
        _merge_kernel,
        grid=(t // tm, nj),
        in_specs=[rows(d)] + [rows(mw)] * 4
        + [pl.BlockSpec((d, tn), (lambda b: (lambda i, j: (0, b * nj + j)))(b)) for b in range(4)]
        + [pl.BlockSpec((1, mw, tn), (lambda b: (lambda i, j: (b, 0, j)))(b)) for b in range(4)],
        out_specs=pl.BlockSpec((tm, tn), lambda i, j: (i, j)),
        out_shape=jax.ShapeDtypeStruct((t, d), BF16),
        compiler_params=_cp(("parallel", "parallel")),
    )(h, *ys, w_gate, w_gate, w_gate, w_gate, w_branch, w_branch, w_branch, w_branch)


def _split_bf16(x):
    hi = x.astype(BF16)
    return hi, (x - hi.astype(F32)).astype(BF16)


def _post1_kernel(x_ref, mix_ref, g1_ref, g2_ref, mod_ref, rw_ref, rb_ref, x1_ref, h2_ref, route_ref, *, ne):
    x = x_ref[...]
    mix = mix_ref[...]
    nm = mix * lax.rsqrt(jnp.mean(mix * mix, axis=-1, keepdims=True) + RMS_EPS) * g1_ref[...]
    x1 = x + mod_ref[0, 2:3, :] * nm
    x1_ref[...] = x1
    h2 = x1 * lax.rsqrt(jnp.mean(x1 * x1, axis=-1, keepdims=True) + RMS_EPS) * g2_ref[...]
    h2 = h2 * (1.0 + mod_ref[0, 4:5, :]) + mod_ref[0, 3:4, :]
    h2_ref[...] = h2
    hh, hl = _split_bf16(h2)
    wh, wl = _split_bf16(rw_ref[...])
    logits = (jnp.dot(hh, wh, preferred_element_type=F32) + jnp.dot(hh, wl, preferred_element_type=F32)
              + jnp.dot(hl, wh, preferred_element_type=F32)) + rb_ref[...]
    tm = logits.shape[0]
    eio = lax.broadcasted_iota(jnp.int32, (tm, ne), 1).astype(F32)
    lane = lax.broadcasted_iota(jnp.int32, (tm, LANES), 1)
    vals, idxs = [], []
    for _ in range(TOP_K):
        m = jnp.max(logits, axis=-1, keepdims=True)
        idx = jnp.min(jnp.where(logits == m, eio, float(ne)), axis=-1, keepdims=True)
        vals.append(m)
        idxs.append(idx)
        logits = jnp.where(eio == idx, -jnp.inf, logits)
    es = [jnp.exp(v - vals[0]) for v in vals]
    den = es[0] + es[1] + es[2] + es[3]
    out = jnp.zeros((tm, LANES), F32)
    for kk in range(TOP_K):
        out = jnp.where(lane == kk, idxs[kk], out)
        out = jnp.where(lane == TOP_K + kk, es[kk] / den, out)
    route_ref[...] = out


def _post1(x, mix, g1, g2, mod, router_w, router_b, tok):
    t, d = x.shape
    ne = router_w.shape[1]
    tm = min(tok.tm, 256)
    rows = pl.BlockSpec((tm, d), lambda i: (i, 0))
    vec = pl.BlockSpec((1, d), lambda i: (0, 0))
    return pl.pallas_call(
        functools.partial(_post1_kernel, ne=ne),
        grid=(t // tm,),
        in_specs=[rows, rows, vec, vec, pl.BlockSpec((1, 6, d), tok.mod_index(tm)),
                  pl.BlockSpec((d, ne), lambda i: (0, 0)), pl.BlockSpec((1, ne), lambda i: (0, 0))],
        out_specs=[rows, rows, pl.BlockSpec((tm, LANES), lambda i: (i, 0))],
        out_shape=[jax.ShapeDtypeStruct((t, d), F32), jax.ShapeDtypeStruct((t, d), F32),
                   jax.ShapeDtypeStruct((t, LANES), F32)],
        compiler_params=_cp(("parallel",)),
    )(x, mix, g1.reshape(1, d), g2.reshape(1, d), mod, router_w, router_b.reshape(1, ne))


def _dispatch_kernel(tok_ref, h_hbm, o_ref, buf_ref, sem, *, bm):
    i = pl.program_id(0)

    def issue(j, carry):
        tk = tok_ref[i, j]
        pltpu.make_async_copy(h_hbm.at[pl.ds(tk, 1), :], buf_ref.at[pl.ds(j, 1), :], sem).start()
        return carry

    lax.fori_loop(0, bm, issue, 0)
    pltpu.make_async_copy(h_hbm.at[pl.ds(0, bm), :], buf_ref, sem).wait()
    o_ref[...] = buf_ref[...].astype(o_ref.dtype)


def _dispatch(h2, tok_pad, bm):
    t, d = h2.shape
    nblk = tok_pad.shape[0]
    return pl.pallas_call(
        functools.partial(_dispatch_kernel, bm=bm),
        grid_spec=pltpu.PrefetchScalarGridSpec(
            num_scalar_prefetch=1,
            grid=(nblk,),
            in_specs=[pl.BlockSpec(memory_space=pl.ANY)],
            out_specs=pl.BlockSpec((bm, d), lambda i, tk: (i, 0)),
            scratch_shapes=[pltpu.VMEM((bm, d), F32), pltpu.SemaphoreType.DMA(())]),
        out_shape=jax.ShapeDtypeStruct((nblk * bm, d), BF16),
        compiler_params=_cp(("arbitrary",)),
    )(tok_pad, h2)


def _ffn_kernel(be_ref, nu_ref, x_ref, wg_ref, wl_ref, bg_ref, bl_ref, w2_ref, b2_ref, o_ref, acc_ref, *, nf):
    m = pl.program_id(0)
    f = pl.program_id(1)
    used = m < nu_ref[0]

    @pl.when(used)
    def _():
        x = x_ref[...]
        glu = jnp.dot(x, wg_ref[0], preferred_element_type=F32) + bg_ref[0]
        lin = jnp.dot(x, wl_ref[0], preferred_element_type=F32) + bl_ref[0]
        glu = jnp.minimum(glu, SWIGLU_LIMIT)
        lin = jnp.clip(lin, -SWIGLU_LIMIT, SWIGLU_LIMIT)
        act = glu * jax.nn.sigmoid(SWIGLU_ALPHA * glu) * (lin + 1.0)
        part = jnp.dot(act.astype(BF16), w2_ref[0], preferred_element_type=F32)

        @pl.when(f == 0)
        def _():
            acc_ref[...] = part + b2_ref[0]

        @pl.when(f > 0)
        def _():
            acc_ref[...] += part

        @pl.when(f == nf - 1)
        def _():
            o_ref[...] = acc_ref[...]

    @pl.when(jnp.logical_not(used) & (f == nf - 1))
    def _():
        o_ref[...] = jnp.zeros_like(o_ref)


def _expert_ffn(xs, blk_e, n_used, w1, b1, w2, b2, bm):
    nr, d = xs.shape
    ne, ff, _ = w2.shape
    nblk = nr // bm
    tf = _div_tile(ff, 256, LANES)
    nf = ff // tf
    return pl.pallas_call(
        functools.partial(_ffn_kernel, nf=nf),
        grid_spec=pltpu.PrefetchScalarGridSpec(
            num_scalar_prefetch=2,
            grid=(nblk, nf),
            in_specs=[pl.BlockSpec((bm, d), lambda m, f, be, nu: (m, 0)),
                      pl.BlockSpec((1, d, tf), lambda m, f, be, nu: (be[m], 0, f)),
                      pl.BlockSpec((1, d, tf), lambda m, f, be, nu: (be[m], 0, nf + f)),
                      pl.BlockSpec((1, 1, tf), lambda m, f, be, nu: (be[m], 0, f)),
                      pl.BlockSpec((1, 1, tf), lambda m, f, be, nu: (be[m], 0, nf + f)),
                      pl.BlockSpec((1, tf, d), lambda m, f, be, nu: (be[m], f, 0)),
                      pl.BlockSpec((1, 1, d), lambda m, f, be, nu: (be[m], 0, 0))],
            out_specs=pl.BlockSpec((bm, d), lambda m, f, be, nu: (m, 0)),
            scratch_shapes=[pltpu.VMEM((bm, d), F32)]),
        out_shape=jax.ShapeDtypeStruct((nr, d), F32),
        compiler_params=_cp(("arbitrary", "arbitrary")),
    )(blk_e, n_used, xs, w1, w1, b1.reshape(ne, 1, 2 * ff), b1.reshape(ne, 1, 2 * ff), w2, b2.reshape(ne, 1, d))


def _combine_kernel(pos_ref, ys_hbm, route_ref, x1_ref, g_ref, mod_ref, o_ref, buf_ref, sem, *, tm, pw):
    i = pl.program_id(0)

    def issue(j, carry):
        for kk in range(TOP_K):
            a = (i * tm + j) * TOP_K + kk
            p = pos_ref[a // pw, a % pw]
            pltpu.make_async_copy(ys_hbm.at[pl.ds(p, 1), :], buf_ref.at[kk, pl.ds(j, 1), :], sem).start()
        return carry

    lax.fori_loop(0, tm, issue, 0)
    for kk in range(TOP_K):
        pltpu.make_async_copy(ys_hbm.at[pl.ds(0, tm), :], buf_ref.at[kk], sem).wait()
    route = route_ref[...]
    ff = None
    for kk in range(TOP_K):
        term = route[:, TOP_K + kk:TOP_K + kk + 1] * buf_ref[kk]
        ff = term if ff is None else ff + term
    nm = ff * lax.rsqrt(jnp.mean(ff * ff, axis=-1, keepdims=True) + RMS_EPS) * g_ref[...]
    o_ref[...] = x1_ref[...] + mod_ref[0, 5:6, :] * nm


def _combine(ys, pos, route, x1, g3, mod, tok):
    t, d = x1.shape
    tm = min(tok.tm, 128)
    pw = pos.shape[1]
    return pl.pallas_call(
        functools.partial(_combine_kernel, tm=tm, pw=pw),
        grid_spec=pltpu.PrefetchScalarGridSpec(
            num_scalar_prefetch=1,
            grid=(t // tm,),
            in_specs=[pl.BlockSpec(memory_space=pl.ANY),
                      pl.BlockSpec((tm, LANES), lambda i, p: (i, 0)),
                      pl.BlockSpec((tm, d), lambda i, p: (i, 0)),
                      pl.BlockSpec((1, d), lambda i, p: (0, 0)),
                      pl.BlockSpec((1, 6, d), lambda i, p: tok.mod_index(tm)(i))],
            out_specs=pl.BlockSpec((tm, d), lambda i, p: (i, 0)),
            scratch_shapes=[pltpu.VMEM((TOP_K, tm, d), F32), pltpu.SemaphoreType.DMA(())]),
        out_shape=jax.ShapeDtypeStruct((t, d), F32),
        compiler_params=_cp(("arbitrary",)),
    )(pos, ys, route, x1, g3.reshape(1, d), mod)


def _moe(h2, route, x1, g3, mod, P, tok):
    t, d = h2.shape
    ne = P["router_w"].shape[1]
    n_assign = t * TOP_K
    bm = min(512, n_assign // 8)
    nblk = -(-n_assign // bm) + ne
    e_flat = route[:, :TOP_K].astype(jnp.int32).reshape(-1)
    order = jnp.argsort(e_flat)
    e_s = e_flat[order]
    counts = jnp.bincount(e_flat, length=ne)
    padded = (counts + bm - 1) // bm * bm
    start = jnp.cumsum(counts) - counts
    pend = jnp.cumsum(padded)
    pstart = pend - padded
    dest = (pstart[e_s] + jnp.arange(n_assign) - start[e_s]).astype(jnp.int32)
    tok_pad = jnp.zeros((nblk * bm,), jnp.int32).at[dest].set((order // TOP_K).astype(jnp.int32))
    pos = jnp.zeros((n_assign,), jnp.int32).at[order].set(dest)
    blk_e = jnp.minimum(jnp.searchsorted(pend, jnp.arange(nblk) * bm, side="right"), ne - 1).astype(jnp.int32)
    n_used = (pend[-1] // bm).astype(jnp.int32).reshape(1)
    xs = _dispatch(h2, tok_pad.reshape(nblk, bm), bm)
    ys = _expert_ffn(xs, blk_e, n_used, P["moe_w1"].astype(BF16), P["moe_b1"], P["moe_w2"].astype(BF16),
                     P["moe_b2"], bm)
    pw = 512 if n_assign % 512 == 0 else bm
    return _combine(ys, pos.reshape(n_assign // pw, pw), route, x1, g3, mod, tok)


def _layer(x, cond, P, tok, cache):
    d = x.shape[1]
    mw = d // 4
    mod = _ada_mod(cond, P["ada_w"], P["ada_b"]).reshape(cond.shape[0], 6, d)
    h = _norm_mod(x, P["norms"][0], mod, tok, shift_row=0, scale_row=1)
    w_in = P["w_in"].astype(BF16)
    rw_in = P["rwkv_mu"].shape[0]
    kvh = cache["swa_k"].shape[2]
    swa_in = mw + 2 * kvh * HEAD_DIM
    o1, o2 = mw, mw + rw_in
    o3 = o2 + 3 * mw
    o4 = o3 + swa_in
    u_s5 = _matmul(h, w_in[:, :o1], tm=tok.tm)
    z_rw = _matmul(h, w_in[:, o1:o2], tm=tok.tm)
    z_na = _matmul(h, w_in[:, o2:o3], tm=tok.tm)
    z_swa = _matmul(h, w_in[:, o3:o4], tm=tok.tm)
    y_s5, st_s5 = _s5_mixer(u_s5, P, tok, cache["s5"])
    y_rw, st_rw = _rwkv_mixer(z_rw, P, tok, cache["rwkv"])
    y_na, y_swa = _attention(z_na, z_swa, P, tok, cache)
    merged = _merge(h, (y_s5, y_rw, y_na, y_swa), w_in[:, o4:], P["w_branch"].astype(BF16))
    mix = _matmul(merged, P["w_o"].astype(BF16), tm=tok.tm)
    x1, h2, route = _post1(x, mix, P["norms"][1], P["norms"][2], mod, P["router_w"], P["router_b"], tok)
    x2 = _moe(h2, route, x1, P["norms"][3], mod, P, tok)
    h_na = mw // HEAD_DIM
    tc = tok.tc
    ctx_out = (z_na[:tc, mw:2 * mw].reshape(tok.bc, tok.lc, h_na, HEAD_DIM),
               z_na[:tc, 2 * mw:].reshape(tok.bc, tok.lc, h_na, HEAD_DIM),
               z_swa[:tc, mw:mw + kvh * HEAD_DIM].reshape(tok.bc, tok.lc, kvh, HEAD_DIM),
               z_swa[:tc, mw + kvh * HEAD_DIM:].reshape(tok.bc, tok.lc, kvh, HEAD_DIM),
               st_s5, st_rw)
    return x2, ctx_out


_PARAM_NAMES = ("ada_w", "ada_b", "norms", "w_in", "s5_lambda_re", "s5_lambda_im", "s5_log_dt", "s5_b_re", "s5_b_im",
                "s5_c_re", "s5_c_im", "s5_d", "s5_glu_w", "s5_glu_b", "rwkv_mu", "rwkv_w0", "rwkv_w2", "rwkv_a0",
                "rwkv_a2", "rwkv_g2", "rwkv_k_k", "rwkv_k_a", "rwkv_r_k", "rwkv_gn_w", "rwkv_gn_b", "na_rpb",
                "swa_sink", "w_branch", "w_o", "router_w", "router_b", "moe_w1", "moe_b1", "moe_w2", "moe_b2")


def kernel(x_prompt, x_sample, cache_na_k, cache_na_v, cache_swa_k, cache_swa_v, state_s5, state_rwkv, c, c_ctx, ada_w, ada_b, norms, w_in, s5_lambda_re, s5_lambda_im, s5_log_dt, s5_b_re, s5_b_im, s5_c_re, s5_c_im, s5_d, s5_glu_w, s5_glu_b, rwkv_mu, rwkv_w0, rwkv_w2, rwkv_a0, rwkv_a2, rwkv_g2, rwkv_k_k, rwkv_k_a, rwkv_r_k, rwkv_gn_w, rwkv_gn_b, na_rpb, swa_sink, w_branch, w_o, router_w, router_b, moe_w1, moe_b1, moe_w2, moe_b2):
    weights = dict(zip(_PARAM_NAMES, (ada_w, ada_b, norms, w_in, s5_lambda_re, s5_lambda_im, s5_log_dt, s5_b_re,
                                      s5_b_im, s5_c_re, s5_c_im, s5_d, s5_glu_w, s5_glu_b, rwkv_mu, rwkv_w0, rwkv_w2,
                                      rwkv_a0, rwkv_a2, rwkv_g2, rwkv_k_k, rwkv_k_a, rwkv_r_k, rwkv_gn_w, rwkv_gn_b,
                                      na_rpb, swa_sink, w_branch, w_o, router_w, router_b, moe_w1, moe_b1, moe_w2,
                                      moe_b2)))
    bc, lc, d = x_prompt.shape
    bl, ll, _ = x_sample.shape
    depth = ada_w.shape[0]
    tok = _Tokens(bc, lc, bl, ll)
    x = jnp.concatenate([x_prompt.reshape(bc * lc, d), x_sample.reshape(bl * ll, d)], axis=0)
    nrow = 1 + bl
    cond = jnp.concatenate([c_ctx[None, :], c, jnp.zeros((-nrow % 8, d), F32)], axis=0)
    outs = [[] for _ in range(6)]
    for layer in range(depth):
        P = {name: w[layer] for name, w in weights.items()}
        cache = dict(na_k=cache_na_k[:, layer], na_v=cache_na_v[:, layer], swa_k=cache_swa_k[:, layer],
                     swa_v=cache_swa_v[:, layer], s5=state_s5[:, layer], rwkv=state_rwkv[:, layer])
        x, ctx_out = _layer(x, cond, P, tok, cache)
        for acc, val in zip(outs, ctx_out):
            acc.append(val)
    y_prompt = x[:tok.tc].reshape(bc, lc, d)
    y_sample = x[tok.tc:].reshape(bl, ll, d)
    return (y_prompt, y_sample) + tuple(jnp.stack(o, axis=1) for o in outs)
```
